```python
import jax
import jax.numpy as jnp
from jax import lax
import numpy as np

D_MODEL = 1024
BATCH = 2
SEQ = 16384
DEPTH = 2

ROPE_THETA = 10000.0
NORM_EPS = 1e-6
NEG_INF = -1e30

RET_HEADS = 4
RET_DK = D_MODEL // (2 * RET_HEADS)
RET_DV = D_MODEL // (2 * RET_HEADS)
RET_CHUNK = 128
S5_WIDTH = D_MODEL // 2
S5_GROUP_CH = 16
S5_GROUPS = S5_WIDTH // S5_GROUP_CH
S5_STATE = 64
S5_STEP_MIN = 1e-3
S5_STEP_MAX = 1e-1
DIL_PATTERNS = ((128, 1), (512, 4), (2048, 16))
DIL_HEADS = 4
DIL_HEAD_DIM = D_MODEL // 16
MLSTM_HEADS = 4
MLSTM_DH = D_MODEL // 8
MLSTM_CHUNK = 128
N_EXPERTS = 16
EXPERT_FF = 2 * D_MODEL
CAPACITY_FACTOR = 2

EVEN_SPLITS = (RET_HEADS * RET_DK, RET_HEADS * RET_DK, RET_HEADS * RET_DV, RET_HEADS * RET_DV, S5_WIDTH)
EVEN_IN = sum(EVEN_SPLITS)
EVEN_OUT = RET_HEADS * RET_DV + S5_WIDTH
DIL_WIDTH = len(DIL_PATTERNS) * DIL_HEADS * DIL_HEAD_DIM
MLSTM_WIDTH = MLSTM_HEADS * MLSTM_DH
ODD_SPLITS = (DIL_WIDTH, DIL_WIDTH, DIL_WIDTH, MLSTM_WIDTH, MLSTM_WIDTH, MLSTM_WIDTH, MLSTM_WIDTH, 4 * MLSTM_HEADS)
ODD_IN = sum(ODD_SPLITS)
ODD_OUT = DIL_HEADS * DIL_HEAD_DIM + MLSTM_WIDTH
N_EVEN = (DEPTH + 1) // 2
N_ODD = DEPTH // 2

kernel_name = 'hybrid_retention_s5_dilated_mlstm_ecmoe'


def _split(z, sizes):
    offsets = np.cumsum(sizes)[:-1].tolist()
    return jnp.split(z, offsets, axis=-1)


def rmsnorm(x, g):
    xf = x.astype(jnp.float32)
    y = xf * lax.rsqrt(jnp.mean(xf * xf, axis=-1, keepdims=True) + NORM_EPS)
    return (y * g.astype(jnp.float32)).astype(x.dtype)


def head_norm(h):
    hf = h.astype(jnp.float32)
    mu = jnp.mean(hf, axis=-1, keepdims=True)
    var = jnp.mean(jnp.square(hf - mu), axis=-1, keepdims=True)
    return (hf - mu) * lax.rsqrt(var + NORM_EPS)


def rotary(x, positions):
    e = x.shape[-1]
    half = e // 2
    inv_freq = ROPE_THETA ** (-jnp.arange(half, dtype=jnp.float32) / half)
    ang = positions.astype(jnp.float32)[:, None] * inv_freq[None, :]
    cos = jnp.cos(ang)[:, None, :]
    sin = jnp.sin(ang)[:, None, :]
    x1 = x[..., :half].astype(jnp.float32)
    x2 = x[..., half:].astype(jnp.float32)
    return jnp.concatenate([x1 * cos - x2 * sin, x1 * sin + x2 * cos], axis=-1).astype(x.dtype)


def retention_direction(q, k, v, log_gamma, strict):
    b, h, l, dk = q.shape
    dv = v.shape[-1]
    c = RET_CHUNK
    n = l // c
    qc = q.reshape(b, h, n, c, dk)
    kc = k.reshape(b, h, n, c, dk)
    vc = v.reshape(b, h, n, c, dv)
    idx = jnp.arange(c)
    diff = idx[:, None] - idx[None, :]
    mask = (diff > 0) if strict else (diff >= 0)
    decay = jnp.where(mask, jnp.exp(jnp.where(mask, diff, 0).astype(jnp.float32)[None] * log_gamma[:, None, None]), 0.0)
    scores = jnp.einsum('bhnid,bhnjd->bhnij', qc, kc) * decay[None, :, None]
    o_intra = jnp.einsum('bhnij,bhnje->bhnie', scores, vc)
    k_w = jnp.exp((c - 1 - idx).astype(jnp.float32)[None, :] * log_gamma[:, None])
    chunk_kv = jnp.einsum('bhnjd,hj,bhnje->bhnde', kc, k_w, vc)
    chunk_decay = jnp.exp(c * log_gamma)[None, :, None, None]

    def step(state, kv_n):
        return chunk_decay * state + kv_n, state

    _, state_before = lax.scan(step, jnp.zeros((b, h, dk, dv), jnp.float32), jnp.moveaxis(chunk_kv, 2, 0))
    state_before = jnp.moveaxis(state_before, 0, 2)
    q_w = jnp.exp((idx + 1).astype(jnp.float32)[None, :] * log_gamma[:, None])
    o_cross = jnp.einsum('bhnid,hi,bhnde->bhnie', qc, q_w, state_before)
    return (o_intra + o_cross).reshape(b, h, l, dv)


def retention_mixer(q, k, v, g, positions):
    b, l, _ = q.shape
    qh = rotary(q.reshape(b, l, RET_HEADS, RET_DK), positions)
    kh = rotary(k.reshape(b, l, RET_HEADS, RET_DK), positions) * (RET_DK ** -0.5)
    vh = v.reshape(b, l, RET_HEADS, RET_DV)
    qh, kh, vh = (t.transpose(0, 2, 1, 3) for t in (qh, kh, vh))
    log_gamma = jnp.log1p(-(2.0 ** (-5.0 - jnp.arange(RET_HEADS, dtype=jnp.float32))))
    fwd = retention_direction(qh, kh, vh, log_gamma, False)
    bwd = jnp.flip(retention_direction(jnp.flip(qh, 2), jnp.flip(kh, 2), jnp.flip(vh, 2), log_gamma, True), 2)
    o = head_norm(fwd + bwd).transpose(0, 2, 1, 3).reshape(b, l, RET_HEADS * RET_DV)
    return (jax.nn.silu(g.astype(jnp.float32)) * o).astype(g.dtype)


def _linear_combine(left, right):
    a1, b1 = left
    a2, b2 = right
    return a1 * a2, a2 * b1 + b2


def s5_mixer(u, a_re, a_im, b_re, b_im, c_re, c_im, log_step, d_skip, glu_w, glu_b):
    b, l, w = u.shape
    uf = u.astype(jnp.float32)
    ug = uf.reshape(b, l, S5_GROUPS, S5_GROUP_CH).astype(jnp.complex64)
    y = d_skip.astype(jnp.float32) * uf
    for direction in range(2):
        lam = lax.complex(a_re[direction].astype(jnp.float32), a_im[direction].astype(jnp.float32))
        step = jnp.exp(log_step[direction].astype(jnp.float32))[:, None]
        a_bar = jnp.exp(lam * step)
        bmat = lax.complex(b_re[direction].astype(jnp.float32), b_im[direction].astype(jnp.float32))
        b_bar = ((a_bar - 1.0) / lam)[..., None] * bmat
        bu = jnp.einsum('gpc,blgc->blgp', b_bar, ug)
        a_seq = jnp.broadcast_to(a_bar, bu.shape)
        _, states = lax.associative_scan(_linear_combine, (a_seq, bu), axis=1, reverse=(direction == 1))
        cmat = lax.complex(c_re[direction].astype(jnp.float32), c_im[direction].astype(jnp.float32))
        y = y + jnp.real(jnp.einsum('gcp,blgp->blgc', cmat, states)).reshape(b, l, w)
    y = jax.nn.gelu(y)
    y = y * jax.nn.sigmoid(y @ glu_w.astype(jnp.float32) + glu_b.astype(jnp.float32))
    return y.astype(u.dtype)


def dilated_window_attention(q, k, v, dilation, half):
    b, l, h, e = q.shape
    blk = half
    unit = dilation * blk
    lp = -(-l // unit) * unit
    j = lp // dilation
    nb = j // blk

    def to_classes(t):
        t = jnp.pad(t, ((0, 0), (0, lp - l), (0, 0), (0, 0)))
        return t.reshape(b, j, dilation, h, e).transpose(0, 2, 3, 1, 4)

    def windows(t):
        t = jnp.pad(t, ((0, 0), (0, 0), (0, 0), (blk, blk), (0, 0))).reshape(b, dilation, h, nb + 2, blk, e)
        return jnp.concatenate([t[:, :, :, :-2], t[:, :, :, 1:-1], t[:, :, :, 2:]], axis=-2)

    qc = to_classes(q).reshape(b, dilation, h, nb, blk, e)
    kw = windows(to_classes(k))
    vw = windows(to_classes(v))
    valid = (jnp.arange(lp) < l).reshape(j, dilation).T
    valid = jnp.pad(valid, ((0, 0), (blk, blk))).reshape(dilation, nb + 2, blk)
    kvalid = jnp.concatenate([valid[:, :-2], valid[:, 1:-1], valid[:, 2:]], axis=-1)
    rel = jnp.arange(3 * blk)[None, :] - blk - jnp.arange(blk)[:, None]
    mask = (jnp.abs(rel) <= half)[None, None] & kvalid[:, :, None, :]
    s = jnp.einsum('brhnie,brhnte->brhnit', qc, kw).astype(jnp.float32) * (e ** -0.5)
    s = jnp.where(mask[None, :, None], s, NEG_INF)
    m = jnp.max(s, axis=-1, keepdims=True)
    p = jnp.exp(s - m)
    den = jnp.sum(p, axis=-1)
    o = jnp.einsum('brhnit,brhnte->brhnie', p, vw) / den[..., None]
    lse = m[..., 0] + jnp.log(den)
    o = o.reshape(b, dilation, h, j, e).transpose(0, 3, 1, 2, 4).reshape(b, lp, h, e)[:, :l]
    lse = lse.reshape(b, dilation, h, j).transpose(0, 3, 1, 2).reshape(b, lp, h)[:, :l]
    return o, lse


def dilated_mixer(q, k, v, positions):
    b, l, _ = q.shape
    n_heads = len(DIL_PATTERNS) * DIL_HEADS
    qh = rotary(q.reshape(b, l, n_heads, DIL_HEAD_DIM), positions)
    kh = rotary(k.reshape(b, l, n_heads, DIL_HEAD_DIM), positions)
    vh = v.reshape(b, l, n_heads, DIL_HEAD_DIM)
    outs, lses = [], []
    for gi, (window, dilation) in enumerate(DIL_PATTERNS):
        sl = slice(gi * DIL_HEADS, (gi + 1) * DIL_HEADS)
        o, lse = dilated_window_attention(qh[:, :, sl], kh[:, :, sl], vh[:, :, sl], dilation, window // (2 * dilation))
        outs.append(o)
        lses.append(lse)
    wts = jax.nn.softmax(jnp.stack(lses), axis=0)
    o = jnp.sum(wts[..., None] * jnp.stack(outs), axis=0)
    return o.reshape(b, l, DIL_HEADS * DIL_HEAD_DIM).astype(q.dtype)


def mlstm_direction(q, k, v, ig, lf):
    b, h, l, dk = q.shape
    dv = v.shape[-1]
    lc = MLSTM_CHUNK
    n = l // lc

    def chunks(t):
        return jnp.moveaxis(t.reshape((b, h, n, lc) + t.shape[3:]), 2, 0)

    causal = jnp.tril(jnp.ones((lc, lc), dtype=bool))

    def step(carry, inp):
        cmem, nvec, m = carry
        qn, kn, vn, ign, lfn = inp
        bcum = jnp.cumsum(lfn, axis=-1)
        log_d = jnp.where(causal, bcum[..., :, None] - bcum[..., None, :] + ign[..., None, :], -jnp.inf)
        log_inter = bcum + m[..., None]
        m_i = jnp.maximum(log_inter, jnp.max(log_d, axis=-1))
        s = jnp.einsum('bhid,bhjd->bhij', qn, kn) * jnp.exp(log_d - m_i[..., None])
        w_inter = jnp.exp(log_inter - m_i)
        num = jnp.einsum('bhij,bhje->bhie', s, vn) + w_inter[..., None] * jnp.einsum('bhid,bhde->bhie', qn, cmem)
        den_raw = jnp.sum(s, axis=-1) + w_inter * jnp.einsum('bhid,bhd->bhi', qn, nvec)
        hout = num / jnp.maximum(jnp.abs(den_raw), jnp.exp(-m_i))[..., None]
        b_last = bcum[..., -1]
        log_w = b_last[..., None] - bcum + ign
        m_new = jnp.maximum(b_last + m, jnp.max(log_w, axis=-1))
        w_old = jnp.exp(b_last + m - m_new)
        w_j = jnp.exp(log_w - m_new[..., None])
        c_new = w_old[..., None, None] * cmem + jnp.einsum('bhj,bhjd,bhje->bhde', w_j, kn, vn)
        n_new = w_old[..., None] * nvec + jnp.einsum('bhj,bhjd->bhd', w_j, kn)
        return (c_new, n_new, m_new), hout

    init = (jnp.zeros((b, h, dk, dv), jnp.float32), jnp.zeros((b, h, dk), jnp.float32), jnp.zeros((b, h), jnp.float32))
    _, hs = lax.scan(step, init, (chunks(q), chunks(k), chunks(v), chunks(ig), chunks(lf)))
    return jnp.moveaxis(hs, 0, 2).reshape(b, h, l, dv)


def mlstm_mixer(q, k, v, o, gates, i_bias, f_bias):
    b, l, _ = q.shape

    def to_heads(t):
        return t.reshape(b, l, MLSTM_HEADS, MLSTM_DH).transpose(0, 2, 1, 3)

    qh, kh, vh = to_heads(q), to_heads(k) * (MLSTM_DH ** -0.5), to_heads(v)
    g = gates.astype(jnp.float32).reshape(b, l, 4, MLSTM_HEADS).transpose(2, 0, 3, 1)
    ig = g[0:2] + i_bias.astype(jnp.float32)[:, None, :, None]
    lf = jax.nn.log_sigmoid(g[2:4] + f_bias.astype(jnp.float32)[:, None, :, None])
    h_f = mlstm_direction(qh, kh, vh, ig[0], lf[0])
    h_b = jnp.flip(mlstm_direction(jnp.flip(qh, 2), jnp.flip(kh, 2), jnp.flip(vh, 2), jnp.flip(ig[1], 2), jnp.flip(lf[1], 2)), 2)
    hn = head_norm(h_f + h_b).transpose(0, 2, 1, 3).reshape(b, l, MLSTM_WIDTH)
    return (jax.nn.sigmoid(o.astype(jnp.float32)) * hn).astype(q.dtype)


def even_mixer(h, w_in, w_out, a_re, a_im, b_re, b_im, c_re, c_im, log_step, d_skip, glu_w, glu_b, positions):
    q, k, v, g, u = _split(h @ w_in, EVEN_SPLITS)
    o_ret = retention_mixer(q, k, v, g, positions)
    o_s5 = s5_mixer(u, a_re, a_im, b_re, b_im, c_re, c_im, log_step, d_skip, glu_w, glu_b)
    return jnp.concatenate([o_ret, o_s5], axis=-1) @ w_out


def odd_mixer(h, w_in, w_out, i_bias, f_bias, positions):
    cq, ck, cv, mq, mk, mv, mo, mg = _split(h @ w_in, ODD_SPLITS)
    o_dil = dilated_mixer(cq, ck, cv, positions)
    o_ml = mlstm_mixer(mq, mk, mv, mo, mg, i_bias, f_bias)
    return jnp.concatenate([o_dil, o_ml], axis=-1) @ w_out


def expert_choice_moe(h, w_router, w1, w3, w2):
    b, t, d = h.shape
    cap = CAPACITY_FACTOR * t // N_EXPERTS
    aff = jax.nn.softmax(jnp.einsum('btd,de->bte', h, w_router).astype(jnp.float32), axis=-1)
    gate, tok = lax.top_k(jnp.swapaxes(aff, 1, 2), cap)
    flat = tok + (jnp.arange(b) * t)[:, None, None]
    xe = h.reshape(b * t, d)[flat]
    a = jnp.einsum('becd,edf->becf', xe, w1)
    g = jnp.einsum('becd,edf->becf', xe, w3)
    ye = jnp.einsum('becf,efd->becd', jax.nn.silu(a) * g, w2) * gate[..., None].astype(h.dtype)
    out = jnp.zeros((b * t, d), ye.dtype).at[flat.reshape(-1)].add(ye.reshape(-1, d))
    return out.reshape(b, t, d).astype(h.dtype)


def setup_inputs(seed: int = 0) -> dict:
    key = jax.random.key(seed)
    ks = jax.random.split(key, 24)

    def nrm(k, shape, scale):
        return jax.random.normal(k, shape, jnp.float32) * scale

    g_, p_, c_ = S5_GROUPS, S5_STATE, S5_GROUP_CH
    x = nrm(ks[0], (BATCH, SEQ, D_MODEL), 1.0)
    norm_mix_g = 1.0 + nrm(ks[1], (DEPTH, D_MODEL), 0.01)
    norm_ffn_g = 1.0 + nrm(ks[2], (DEPTH, D_MODEL), 0.01)
    final_g = 1.0 + nrm(ks[3], (D_MODEL,), 0.01)
    ev_w_in = nrm(ks[4], (N_EVEN, D_MODEL, EVEN_IN), D_MODEL ** -0.5)
    ev_w_out = nrm(ks[5], (N_EVEN, EVEN_OUT, D_MODEL), EVEN_OUT ** -0.5)
    s5_a_re = -0.5 + nrm(ks[6], (N_EVEN, 2, g_, p_), 0.01)
    s5_a_im = jnp.pi * jnp.arange(p_, dtype=jnp.float32) + nrm(ks[7], (N_EVEN, 2, g_, p_), 0.01)
    s5_b_re = nrm(ks[8], (N_EVEN, 2, g_, p_, c_), (2 * c_) ** -0.5)
    s5_b_im = nrm(ks[9], (N_EVEN, 2, g_, p_, c_), (2 * c_) ** -0.5)
    s5_c_re = nrm(ks[10], (N_EVEN, 2, g_, c_, p_), (2 * p_) ** -0.5)
    s5_c_im = nrm(ks[11], (N_EVEN, 2, g_, c_, p_), (2 * p_) ** -0.5)
    s5_log_step = np.log(S5_STEP_MIN) + jax.random.uniform(ks[12], (N_EVEN, 2, g_), jnp.float32) * (np.log(S5_STEP_MAX) - np.log(S5_STEP_MIN))
    s5_d = nrm(ks[13], (N_EVEN, S5_WIDTH), 1.0)
    s5_glu_w = nrm(ks[14], (N_EVEN, S5_WIDTH, S5_WIDTH), S5_WIDTH ** -0.5)
    s5_glu_b = nrm(ks[15], (N_EVEN, S5_WIDTH), 0.01)
    od_w_in = nrm(ks[16], (N_ODD, D_MODEL, ODD_IN), D_MODEL ** -0.5)
    od_w_out = nrm(ks[17], (N_ODD, ODD_OUT, D_MODEL), ODD_OUT ** -0.5)
    ml_i_bias = nrm(ks[18], (N_ODD, 2, MLSTM_HEADS), 0.1)
    ml_f_bias = jnp.linspace(3.0, 6.0, MLSTM_HEADS, dtype=jnp.float32) + nrm(ks[19], (N_ODD, 2, MLSTM_HEADS), 0.01)
    moe_router = nrm(ks[20], (DEPTH, D_MODEL, N_EXPERTS), D_MODEL ** -0.5)
    moe_w1 = nrm(ks[21], (DEPTH, N_EXPERTS, D_MODEL, EXPERT_FF), D_MODEL ** -0.5)
    moe_w3 = nrm(ks[22], (DEPTH, N_EXPERTS, D_MODEL, EXPERT_FF), D_MODEL ** -0.5)
    moe_w2 = nrm(ks[23], (DEPTH, N_EXPERTS, EXPERT_FF, D_MODEL), EXPERT_FF ** -0.5)
    return {'x': x, 'norm_mix_g': norm_mix_g, 'norm_ffn_g': norm_ffn_g, 'final_g': final_g,
            'ev_w_in': ev_w_in, 'ev_w_out': ev_w_out, 's5_a_re': s5_a_re, 's5_a_im': s5_a_im,
            's5_b_re': s5_b_re, 's5_b_im': s5_b_im, 's5_c_re': s5_c_re, 's5_c_im': s5_c_im,
            's5_log_step': s5_log_step, 's5_d': s5_d, 's5_glu_w': s5_glu_w, 's5_glu_b': s5_glu_b,
            'od_w_in': od_w_in, 'od_w_out': od_w_out, 'ml_i_bias': ml_i_bias, 'ml_f_bias': ml_f_bias,
            'moe_router': moe_router, 'moe_w1': moe_w1, 'moe_w3': moe_w3, 'moe_w2': moe_w2}


def reference(x, norm_mix_g, norm_ffn_g, final_g, ev_w_in, ev_w_out, s5_a_re, s5_a_im, s5_b_re, s5_b_im,
              s5_c_re, s5_c_im, s5_log_step, s5_d, s5_glu_w, s5_glu_b, od_w_in, od_w_out, ml_i_bias, ml_f_bias,
              moe_router, moe_w1, moe_w3, moe_w2):
    positions = jnp.arange(x.shape[1])
    for layer in range(DEPTH):
        h = rmsnorm(x, norm_mix_g[layer])
        if layer % 2 == 0:
            e = layer // 2
            mix = even_mixer(h, ev_w_in[e], ev_w_out[e], s5_a_re[e], s5_a_im[e], s5_b_re[e], s5_b_im[e],
                             s5_c_re[e], s5_c_im[e], s5_log_step[e], s5_d[e], s5_glu_w[e], s5_glu_b[e], positions)
        else:
            o = layer // 2
            mix = odd_mixer(h, od_w_in[o], od_w_out[o], ml_i_bias[o], ml_f_bias[o], positions)
        x = x + mix
        x = x + expert_choice_moe(rmsnorm(x, norm_ffn_g[layer]), moe_router[layer], moe_w1[layer], moe_w3[layer], moe_w2[layer])
    return rmsnorm(x, final_g)
```

```python
import functools
import math

import numpy as np
import jax
import jax.numpy as jnp
from jax import lax
from jax.experimental import pallas as pl
from jax.experimental.pallas import tpu as pltpu

F32 = jnp.float32
BF16 = jnp.bfloat16
I32 = jnp.int32

D_MODEL = 1024
ROPE_THETA = 10000.0
NORM_EPS = 1e-6
NEG_INF = -1e30

RET_HEADS = 4
RET_DK = 128
S5_WIDTH = 512
S5_GROUP_CH = 16
S5_GROUPS = 32
S5_STATE = 64
DIL_PATTERNS = ((128, 1), (512, 4), (2048, 16))
DIL_HEADS = 4
DIL_HEAD_DIM = 64
DIL_HALF = 64
MLSTM_HEADS = 4
MLSTM_DH = 128
N_EXPERTS = 16
EXPERT_FF = 2048
CAPACITY_FACTOR = 2

LANES = 128
BF16_ROWS = 16
MIB = 1024 * 1024

ROW_TILE = 512
RET_CHUNK = 256
MLSTM_CHUNK = 256
S5_CHUNK = 32
DIL_TILE = 256
DIL_SUB = 128
FFN_F_TILE = 512
FFN_ROW_BLOCK = 256
CMB_TILE = 256

NT_DIMS = (((1,), (1,)), ((), ()))
TN_DIMS = (((0,), (0,)), ((), ()))
HI = lax.Precision.HIGHEST


def _cparams(n_grid, vmem_mib):
    return pltpu.CompilerParams(dimension_semantics=("arbitrary",) * n_grid,
                                vmem_limit_bytes=vmem_mib * MIB)


def _rms(x, g):
    return x * lax.rsqrt(jnp.mean(x * x, axis=-1, keepdims=True) + NORM_EPS) * g


def _head_norm(h):
    mu = jnp.mean(h, axis=-1, keepdims=True)
    d = h - mu
    return d * lax.rsqrt(jnp.mean(d * d, axis=-1, keepdims=True) + NORM_EPS)


def _sigmoid(x):
    return 1.0 / (1.0 + jnp.exp(-x))


def _log_sigmoid(x):
    return jnp.minimum(x, 0.0) - jnp.log1p(jnp.exp(-jnp.abs(x)))


def _rope_tables(seq, half, reps):
    inv_freq = ROPE_THETA ** (-jnp.arange(half, dtype=F32) / half)
    ang = jnp.arange(seq).astype(F32)[:, None] * inv_freq[None, :]
    c, s = jnp.cos(ang), jnp.sin(ang)
    return jnp.concatenate([c, c] * reps, axis=1), jnp.concatenate([-s, s] * reps, axis=1)


def _rotate_half(y, half):
    if 2 * half == LANES:
        return pltpu.roll(y, half, 1)
    lane = lax.broadcasted_iota(I32, y.shape, 1)
    return jnp.where(lane % (2 * half) < half, pltpu.roll(y, LANES - half, 1), pltpu.roll(y, half, 1))


def _proj_kernel(x_ref, g_ref, cos_ref, sin_ref, *refs, n_out, rope_half, rope_outs, scales):
    w_refs, o_refs = refs[:n_out], refs[n_out:]
    h = _rms(x_ref[...], g_ref[...]).astype(BF16)
    cos, sin = cos_ref[...], sin_ref[...]
    for i in range(n_out):
        y = jnp.dot(h, w_refs[i][...], preferred_element_type=F32)
        if scales[i] != 1.0:
            y = y * scales[i]
        if i in rope_outs:
            parts = []
            for j in range(y.shape[1] // LANES):
                yj = y[:, j * LANES:(j + 1) * LANES]
                parts.append(yj * cos + _rotate_half(yj, rope_half) * sin)
            y = jnp.concatenate(parts, axis=1)
        o_refs[i][...] = y.astype(o_refs[i].dtype)


def _project(x2, g, cos, sin, weights, out_dtypes, rope_half, rope_outs, scales, seq):
    rows = x2.shape[0]
    tm = min(ROW_TILE, seq)
    n_pos = seq // tm
    n_out = len(weights)
    in_specs = [pl.BlockSpec((tm, D_MODEL), lambda i: (i, 0)),
                pl.BlockSpec((1, D_MODEL), lambda i: (0, 0)),
                pl.BlockSpec((tm, LANES), lambda i: (i % n_pos, 0)),
                pl.BlockSpec((tm, LANES), lambda i: (i % n_pos, 0))]
    in_specs += [pl.BlockSpec(w.shape, lambda i: (0, 0)) for w in weights]
    out_specs = [pl.BlockSpec((tm, w.shape[1]), lambda i: (i, 0)) for w in weights]
    out_shape = [jax.ShapeDtypeStruct((rows, w.shape[1]), dt) for w, dt in zip(weights, out_dtypes)]
    kern = functools.partial(_proj_kernel, n_out=n_out, rope_half=rope_half, rope_outs=rope_outs, scales=scales)
    return pl.pallas_call(kern, grid=(rows // tm,), in_specs=in_specs, out_specs=out_specs,
                          out_shape=out_shape, compiler_params=_cparams(1, 48))(x2, g, cos, sin, *weights)


def _retention_kernel(q_ref, k_ref, v_ref, dec_ref, qw_ref, kw_ref, cd_ref, *refs, final):
    if final:
        prev_ref, gate_ref, o_ref, state = refs
    else:
        o_ref, state = refs

    @pl.when(pl.program_id(1) == 0)
    def _():
        state[...] = jnp.zeros_like(state)

    for h in range(RET_HEADS):
        sl = slice(h * RET_DK, (h + 1) * RET_DK)
        qh, kh, vh = q_ref[:, sl], k_ref[:, sl], v_ref[:, sl]
        s = lax.dot_general(qh, kh, NT_DIMS, preferred_element_type=F32) * dec_ref[h]
        o = jnp.dot(s.astype(BF16), vh, preferred_element_type=F32)
        qs = (qh.astype(F32) * qw_ref[h]).astype(BF16)
        o = o + jnp.dot(qs, state[h].astype(BF16), preferred_element_type=F32)
        ks = (kh.astype(F32) * kw_ref[h]).astype(BF16)
        state[h] = cd_ref[h] * state[h] + lax.dot_general(ks, vh, TN_DIMS, preferred_element_type=F32)
        if final:
            o = _head_norm(o + prev_ref[:, sl])
            gt = gate_ref[:, sl].astype(F32)
            o = gt * _sigmoid(gt) * o
        o_ref[:, sl] = o.astype(o_ref.dtype)


def _retention_tables(reverse, c):
    log_gamma = np.log1p(-(2.0 ** (-5.0 - np.arange(RET_HEADS, dtype=np.float64))))
    idx = np.arange(c)
    diff = idx[:, None] - idx[None, :]
    if reverse:
        mask, dist = diff < 0, -diff
        q_pow, k_pow = c - idx, idx
    else:
        mask, dist = diff >= 0, diff
        q_pow, k_pow = idx + 1, c - 1 - idx
    dec = np.where(mask[None], np.exp(np.where(mask, dist, 0)[None] * log_gamma[:, None, None]), 0.0)
    qw = np.exp(q_pow[None, :] * log_gamma[:, None])
    kw = np.exp(k_pow[None, :] * log_gamma[:, None])
    cd = np.exp(c * log_gamma)
    bc = lambda t: np.broadcast_to(t[:, :, None], (RET_HEADS, c, RET_DK))
    return (jnp.asarray(dec, F32), jnp.asarray(bc(qw), F32), jnp.asarray(bc(kw), F32),
            jnp.asarray(np.broadcast_to(cd[:, None, None], (RET_HEADS, 1, RET_DK)), F32))


def _retention_pass(q, k, v, reverse, prev=None, gate=None):
    b, seq, width = q.shape
    c = min(RET_CHUNK, seq)
    n = seq // c
    final = prev is not None
    dec, qw, kw, cd = _retention_tables(reverse, c)
    tok = (lambda bi, ni: (bi, n - 1 - ni, 0)) if reverse else (lambda bi, ni: (bi, ni, 0))
    tok_spec = pl.BlockSpec((None, c, width), tok)
    const = lambda a: pl.BlockSpec(a.shape, lambda bi, ni: (0,) * a.ndim)
    args = [q, k, v, dec, qw, kw, cd]
    in_specs = [tok_spec, tok_spec, tok_spec, const(dec), const(qw), const(kw), const(cd)]
    if final:
        args += [prev, gate]
        in_specs += [tok_spec, tok_spec]
    return pl.pallas_call(
        functools.partial(_retention_kernel, final=final), grid=(b, n), in_specs=in_specs,
        out_specs=tok_spec, out_shape=jax.ShapeDtypeStruct((b, seq, width), BF16 if final else F32),
        scratch_shapes=[pltpu.VMEM((RET_HEADS, RET_DK, RET_DK), F32)],
        compiler_params=_cparams(2, 32))(*args)


def _s5_tables(a_re, a_im, b_re, b_im, c_re, c_im, log_step, d_skip, n_chunks):
    c, grp, ch, st = S5_CHUNK, S5_GROUPS, S5_GROUP_CH, S5_STATE
    lam = lax.complex(a_re.astype(F32), a_im.astype(F32))
    z = lam * jnp.exp(log_step.astype(F32))[..., None]
    coef = (jnp.exp(z) - 1.0) / lam
    b_bar = coef[..., None] * lax.complex(b_re.astype(F32), b_im.astype(F32))
    cmat = lax.complex(c_re.astype(F32), c_im.astype(F32))
    tau = jnp.arange(c + 1, dtype=F32)
    pw = jnp.exp(tau[:, None, None, None] * z[None])
    kern = jnp.real(jnp.einsum('dgop,tdgp,dgpi->dtgoi', cmat, pw[:c], b_bar, precision=HI))
    jj, ii = np.meshgrid(np.arange(c), np.arange(c), indexing='ij')
    kf = jnp.where((ii >= jj)[:, :, None, None, None], kern[0][np.clip(ii - jj, 0, c - 1)], 0.0)
    kb = jnp.where((jj >= ii)[:, :, None, None, None], kern[1][np.clip(jj - ii, 0, c - 1)], 0.0)
    skip = jnp.eye(ch, dtype=F32)[None] * d_skip.astype(F32).reshape(grp, ch)[:, :, None]
    tk = kf + kb + jnp.asarray(np.eye(c))[:, :, None, None, None].astype(F32) * skip[None, None]
    tmat = tk.transpose(2, 0, 4, 1, 3).reshape(grp, c * ch, c * ch)
    j = np.arange(c)
    pf = pw[c - 1 - j, 0][:, :, None, :] * b_bar[0].transpose(0, 2, 1)[None]
    pb = pw[j, 1][:, :, None, :] * b_bar[1].transpose(0, 2, 1)[None]
    pmat = jnp.concatenate([jnp.real(pf), jnp.imag(pf), jnp.real(pb), jnp.imag(pb)], axis=-1)
    pmat = pmat.transpose(1, 0, 2, 3).reshape(grp, c * ch, 4 * st)
    qf = cmat[0].transpose(0, 2, 1)[:, :, None, :] * pw[j + 1, 0].transpose(1, 2, 0)[:, :, :, None]
    qb = cmat[1].transpose(0, 2, 1)[:, :, None, :] * pw[c - j, 1].transpose(1, 2, 0)[:, :, :, None]
    qmat = jnp.concatenate([jnp.real(qf), -jnp.imag(qf), jnp.real(qb), -jnp.imag(qb)], axis=1)
    qmat = qmat.reshape(grp, 4 * st, c * ch)
    n_steps = max(1, int(math.ceil(math.log2(n_chunks))))
    ak = jnp.exp((c * (2.0 ** jnp.arange(n_steps, dtype=F32)))[:, None, None, None] * z[None])
    are, aim = jnp.real(ak), jnp.imag(ak)
    apow = jnp.stack([jnp.concatenate([are[:, 0], are[:, 0]], -1), jnp.concatenate([-aim[:, 0], aim[:, 0]], -1),
                      jnp.concatenate([are[:, 1], are[:, 1]], -1), jnp.concatenate([-aim[:, 1], aim[:, 1]], -1)],
                     axis=2)
    return tmat.astype(BF16), pmat.astype(BF16), qmat.astype(BF16), apow.transpose(1, 0, 2, 3)


def _s5_kernel(u_ref, t_ref, p_ref, q_ref, a_ref, y_ref, *, n_chunks, n_steps):
    u = u_ref[...]
    z = jnp.dot(u, p_ref[...], preferred_element_type=F32)
    rows = z.shape[0]
    pos = lax.broadcasted_iota(I32, (rows, LANES), 0) % n_chunks
    half = S5_STATE

    def scan(s, reverse, base):
        for kk in range(n_steps):
            sh = 1 << kk
            if sh >= n_chunks:
                break
            if reverse:
                shifted = jnp.where(pos < n_chunks - sh, pltpu.roll(s, rows - sh, 0), 0.0)
            else:
                shifted = jnp.where(pos >= sh, pltpu.roll(s, sh, 0), 0.0)
            a1, a2 = a_ref[kk, base:base + 1, :], a_ref[kk, base + 1:base + 2, :]
            s = s + shifted * a1 + pltpu.roll(shifted, half, 1) * a2
        if reverse:
            return jnp.where(pos < n_chunks - 1, pltpu.roll(s, rows - 1, 0), 0.0)
        return jnp.where(pos >= 1, pltpu.roll(s, 1, 0), 0.0)

    sf = scan(z[:, :LANES], False, 0)
    sb = scan(z[:, LANES:], True, 2)
    st = jnp.concatenate([sf, sb], axis=1).astype(BF16)
    y = jnp.dot(u, t_ref[...], preferred_element_type=F32) + jnp.dot(st, q_ref[...], preferred_element_type=F32)
    y_ref[...] = y.astype(y_ref.dtype)


def _s5_conv(u, tables):
    b, seq, _ = u.shape
    c, grp, ch = S5_CHUNK, S5_GROUPS, S5_GROUP_CH
    n = seq // c
    tmat, pmat, qmat, apow = tables
    n_steps = apow.shape[1]
    ug = u.reshape(b, n, c, grp, ch).transpose(3, 0, 1, 2, 4).reshape(grp, b * n, c * ch)
    grp_spec = lambda a: pl.BlockSpec((None,) + a.shape[1:], lambda g: (g,) + (0,) * (a.ndim - 1))
    yg = pl.pallas_call(
        functools.partial(_s5_kernel, n_chunks=n, n_steps=n_steps), grid=(grp,),
        in_specs=[grp_spec(ug), grp_spec(tmat), grp_spec(pmat), grp_spec(qmat), grp_spec(apow)],
        out_specs=grp_spec(ug), out_shape=jax.ShapeDtypeStruct(ug.shape, BF16),
        compiler_params=_cparams(1, 32))(ug, tmat, pmat, qmat, apow)
    return yg.reshape(grp, b, n, c, ch).transpose(1, 2, 3, 0, 4).reshape(b, seq, grp * ch)


def _router_tail(xn, gf_ref, wr_ref, x_o, h_o, aff_o):
    x_o[...] = xn
    hn = _rms(xn, gf_ref[...])
    h_o[...] = hn
    logits = lax.dot_general(wr_ref[...], hn, NT_DIMS, precision=HI, preferred_element_type=F32)
    e = jnp.exp(logits - jnp.max(logits, axis=0, keepdims=True))
    aff_o[...] = e / jnp.sum(e, axis=0, keepdims=True)


def _out_even_kernel(x_ref, ret_ref, s5_ref, wa_ref, wb_ref, gw_ref, gb_ref, gf_ref, wr_ref, x_o, h_o, aff_o):
    y = jax.nn.gelu(s5_ref[...].astype(F32))
    gl = jnp.dot(y.astype(BF16), gw_ref[...], preferred_element_type=F32) + gb_ref[...]
    y = (y * _sigmoid(gl)).astype(BF16)
    mix = jnp.dot(ret_ref[...], wa_ref[...], preferred_element_type=F32)
    mix = mix + jnp.dot(y, wb_ref[...], preferred_element_type=F32)
    _router_tail(x_ref[...] + mix, gf_ref, wr_ref, x_o, h_o, aff_o)


def _out_odd_kernel(x_ref, o0, o1, o2, l0, l1, l2, ml_ref, wa_ref, wb_ref, gf_ref, wr_ref, x_o, h_o, aff_o):
    lses = [l0[...], l1[...], l2[...]]
    m = jnp.maximum(jnp.maximum(lses[0], lses[1]), lses[2])
    ws = [jnp.exp(l - m) for l in lses]
    num = ws[0] * o0[...].astype(F32) + ws[1] * o1[...].astype(F32) + ws[2] * o2[...].astype(F32)
    dil = (num / (ws[0] + ws[1] + ws[2])).astype(BF16)
    mix = jnp.dot(dil, wa_ref[...], preferred_element_type=F32)
    mix = mix + jnp.dot(ml_ref[...], wb_ref[...], preferred_element_type=F32)
    _router_tail(x_ref[...] + mix, gf_ref, wr_ref, x_o, h_o, aff_o)


def _out_call(kern, x2, acts, consts, seq):
    rows = x2.shape[0]
    tm = min(ROW_TILE, seq)
    n_pos = seq // tm
    row_spec = lambda a: pl.BlockSpec((tm, a.shape[1]), lambda i: (i, 0))
    in_specs = [row_spec(x2)] + [row_spec(a) for a in acts]
    in_specs += [pl.BlockSpec(a.shape, lambda i: (0,) * a.ndim) for a in consts]
    out_specs = [row_spec(x2), row_spec(x2),
                 pl.BlockSpec((None, N_EXPERTS, tm), lambda i: (i // n_pos, 0, i % n_pos))]
    out_shape = [jax.ShapeDtypeStruct(x2.shape, F32), jax.ShapeDtypeStruct(x2.shape, F32),
                 jax.ShapeDtypeStruct((rows // seq, N_EXPERTS, seq), F32)]
    return pl.pallas_call(kern, grid=(rows // tm,), in_specs=in_specs, out_specs=out_specs,
                          out_shape=out_shape, compiler_params=_cparams(1, 48))(x2, *acts, *consts)


def _split_cols(w, sizes):
    offs = np.cumsum((0,) + tuple(sizes))
    return [w[:, offs[i]:offs[i + 1]].astype(BF16) for i in range(len(sizes))]


def _even_mixer_acts(x2, b, seq, norm_g, w_in, s5_params):
    cos, sin = _rope_tables(seq, RET_DK // 2, 1)
    w = _split_cols(w_in, (512, 512, 512, 512, 512))
    q, k, v, gate, u = _project(x2, norm_g.reshape(1, D_MODEL).astype(F32), cos, sin, w, (BF16,) * 5,
                                RET_DK // 2, (0, 1), (1.0, RET_DK ** -0.5, 1.0, 1.0, 1.0), seq)
    sh = lambda t: t.reshape(b, seq, -1)
    fwd = _retention_pass(sh(q), sh(k), sh(v), False)
    o_ret = _retention_pass(sh(q), sh(k), sh(v), True, prev=fwd, gate=sh(gate))
    tables = _s5_tables(*s5_params, n_chunks=seq // S5_CHUNK)
    y_s5 = _s5_conv(sh(u), tables)
    return o_ret.reshape(b * seq, -1), y_s5.reshape(b * seq, -1)


def _moe_select_kernel(aff_ref, pos_ref, idx_ref, offs_ref, *, cap):
    n_blk = aff_ref.shape[1]
    bits = lax.bitcast_convert_type(aff_ref[...], I32)

    def count(mask):
        return jnp.sum(jnp.sum(mask.astype(F32), axis=2, keepdims=True), axis=1, keepdims=True)

    def search(i, thr):
        cand = thr | jnp.left_shift(jnp.int32(1), 30 - i)
        return jnp.where(count(bits >= cand) >= cap, cand, thr)

    thr = lax.fori_loop(0, 31, search, jnp.zeros((N_EXPERTS, 1, 1), I32))
    need = cap - count(bits > thr)

    r_i = lax.broadcasted_iota(I32, (LANES, LANES), 0)
    c_i = lax.broadcasted_iota(I32, (LANES, LANES), 1)
    upper_incl = (r_i <= c_i).astype(BF16)
    rb = lax.broadcasted_iota(I32, (n_blk, n_blk), 0)
    cb = lax.broadcasted_iota(I32, (n_blk, n_blk), 1)
    lower_strict = (cb < rb).astype(BF16)
    eye_blk = rb == cb
    k_col = lax.broadcasted_iota(I32, (cap, 1), 0).astype(F32)
    blk_lane = lax.broadcasted_iota(I32, (cap, n_blk), 1)

    def cumsum(mask_f):
        m16 = mask_f.astype(BF16)
        rowcum = jnp.dot(m16, upper_incl, preferred_element_type=F32)
        offs = jnp.sum(jnp.dot(lower_strict, m16, preferred_element_type=F32), axis=1, keepdims=True)
        return rowcum, offs

    def to_row(col):
        return jnp.sum(jnp.where(eye_blk, col, 0.0), axis=0, keepdims=True)

    for e in range(N_EXPERTS):
        be = bits[e]
        gt, eq = be > thr[e], be == thr[e]
        rowcum, offs = cumsum(eq.astype(F32))
        sel = gt | (eq & (rowcum + offs <= need[e]))
        rowcum, offs = cumsum(sel.astype(F32))
        pos_ref[e] = jnp.where(sel, rowcum + offs - 1.0, -1.0).astype(I32)
        offs_ref[e] = offs.astype(I32)
        end_row = to_row(offs + rowcum[:, LANES - 1:LANES])
        blk = jnp.sum((end_row <= k_col).astype(F32), axis=1, keepdims=True)
        onehot = blk_lane == blk.astype(I32)
        crow = jnp.dot(onehot.astype(BF16), rowcum.astype(BF16), preferred_element_type=F32)
        offk = jnp.sum(jnp.where(onehot, to_row(offs), 0.0), axis=1, keepdims=True)
        within = jnp.sum((crow <= k_col - offk).astype(F32), axis=1, keepdims=True)
        idx_ref[e] = (blk * LANES + within).astype(I32)


def _moe_select(aff_t):
    b, n_exp, seq = aff_t.shape
    n_blk = seq // LANES
    cap = CAPACITY_FACTOR * seq // N_EXPERTS
    blk_spec = lambda last: pl.BlockSpec((None, n_exp, n_blk, last), lambda bi: (bi, 0, 0, 0))
    pos, idx, offs = pl.pallas_call(
        functools.partial(_moe_select_kernel, cap=cap), grid=(b,),
        in_specs=[blk_spec(LANES)],
        out_specs=[blk_spec(LANES), pl.BlockSpec((None, n_exp, cap, 1), lambda bi: (bi, 0, 0, 0)), blk_spec(1)],
        out_shape=[jax.ShapeDtypeStruct((b, n_exp, n_blk, LANES), I32),
                   jax.ShapeDtypeStruct((b, n_exp, cap, 1), I32),
                   jax.ShapeDtypeStruct((b, n_exp, n_blk, 1), I32)],
        compiler_params=_cparams(1, 48))(aff_t.reshape(b, n_exp, n_blk, LANES))
    return pos.reshape(b, n_exp, seq), idx.reshape(b, n_exp, cap), offs.reshape(b, n_exp, n_blk)


def _moe_ffn_kernel(idx_ref, h_hbm, w1_ref, w3_ref, w2_ref, y_ref, xbuf, acc, sem, *, seq, cap):
    bi, f = pl.program_id(0), pl.program_id(2)
    n_f = pl.num_programs(2)

    def row_copy(r):
        return pltpu.make_async_copy(h_hbm.at[pl.ds(bi * seq + idx_ref[0, r], 1)], xbuf.at[pl.ds(r, 1)], sem)

    @pl.when(f == 0)
    def _():
        def start(r, carry):
            row_copy(r).start()
            return carry

        def wait(r, carry):
            row_copy(r).wait()
            return carry

        lax.fori_loop(0, cap, start, 0)
        lax.fori_loop(0, cap, wait, 0)

    w1 = w1_ref[...].astype(BF16)
    w3 = w3_ref[...].astype(BF16)
    w2 = w2_ref[...].astype(BF16)
    blk = min(FFN_ROW_BLOCK, cap)
    for rb in range(cap // blk):
        rows = pl.ds(rb * blk, blk)
        xb = jnp.concatenate([xbuf[rows, s, :] for s in range(xbuf.shape[1])], axis=1).astype(BF16)
        a = jnp.dot(xb, w1, preferred_element_type=F32)
        g = jnp.dot(xb, w3, preferred_element_type=F32)
        mid = (a * _sigmoid(a) * g).astype(BF16)
        part = jnp.dot(mid, w2, preferred_element_type=F32)

        @pl.when(f == 0)
        def _():
            acc[rows, :] = part

        @pl.when(f > 0)
        def _():
            acc[rows, :] += part

    @pl.when(f == n_f - 1)
    def _():
        y_ref[...] = acc[...].astype(y_ref.dtype)


def _moe_ffn(h2, idx, w1, w3, w2, seq):
    b, n_exp, cap = idx.shape
    sub = D_MODEL // LANES
    h3 = h2.reshape(h2.shape[0], sub, LANES)
    ff = w1.shape[2]
    n_f = ff // FFN_F_TILE
    return pl.pallas_call(
        functools.partial(_moe_ffn_kernel, seq=seq, cap=cap), grid=(b, n_exp, n_f),
        in_specs=[pl.BlockSpec((None, 1, cap), lambda bi, e, f: (bi * n_exp + e, 0, 0), memory_space=pltpu.SMEM),
                  pl.BlockSpec(memory_space=pl.ANY),
                  pl.BlockSpec((None, D_MODEL, FFN_F_TILE), lambda bi, e, f: (e, 0, f)),
                  pl.BlockSpec((None, D_MODEL, FFN_F_TILE), lambda bi, e, f: (e, 0, f)),
                  pl.BlockSpec((None, FFN_F_TILE, D_MODEL), lambda bi, e, f: (e, f, 0))],
        out_specs=pl.BlockSpec((None, None, cap, D_MODEL), lambda bi, e, f: (bi, e, 0, 0)),
        out_shape=jax.ShapeDtypeStruct((b, n_exp, cap, D_MODEL), BF16),
        scratch_shapes=[pltpu.VMEM((cap, sub, LANES), F32), pltpu.VMEM((cap, D_MODEL), F32),
                        pltpu.SemaphoreType.DMA],
        compiler_params=_cparams(3, 56))(idx.reshape(b * n_exp, 1, cap), h3, w1, w3, w2)


def _moe_combine_kernel(lo_ref, x_ref, pos_ref, aff_ref, y_hbm, *refs, cap, final):
    if final:
        fg_ref, o_ref, ybuf, sem = refs
    else:
        o_ref, ybuf, sem = refs
    bi, ti = pl.program_id(0), pl.program_id(1)
    n_b, n_t = pl.num_programs(0), pl.num_programs(1)
    step = bi * n_t + ti
    win = ybuf.shape[2]

    def start_of(b2, t2, e):
        lo = lo_ref[(b2 * n_t + t2) * N_EXPERTS + e]
        return pl.multiple_of(jnp.minimum((lo // BF16_ROWS) * BF16_ROWS, cap - win), BF16_ROWS)

    def window_copy(b2, t2, e, slot):
        return pltpu.make_async_copy(y_hbm.at[b2, e, pl.ds(start_of(b2, t2, e), win)], ybuf.at[slot, e],
                                     sem.at[slot, e])

    def issue(b2, t2, slot):
        for e in range(N_EXPERTS):
            window_copy(b2, t2, e, slot).start()

    slot = step % 2

    @pl.when(step == 0)
    def _():
        issue(bi, ti, 0)

    @pl.when(step + 1 < n_b * n_t)
    def _():
        nxt = step + 1
        issue(nxt // n_t, nxt % n_t, 1 - slot)

    acc = x_ref[...]
    lane = lax.broadcasted_iota(I32, (x_ref.shape[0], win), 1)
    for e in range(N_EXPERTS):
        window_copy(bi, ti, e, slot).wait()
        rel = pos_ref[:, e:e + 1] - start_of(bi, ti, e)
        onehot = jnp.where(lane == rel, 1.0, 0.0).astype(BF16)
        contrib = jnp.dot(onehot, ybuf[slot, e], preferred_element_type=F32)
        acc = acc + aff_ref[:, e:e + 1] * contrib
    if final:
        acc = _rms(acc, fg_ref[...])
    o_ref[...] = acc


def _moe_combine(x3, pos_t, aff, lo, y, final_g=None):
    b, seq, _ = x3.shape
    cap = y.shape[2]
    tile = min(CMB_TILE, cap)
    n_t = seq // tile
    final = final_g is not None
    tok = lambda w: pl.BlockSpec((None, tile, w), lambda bi, ti, lo_ref: (bi, ti, 0))
    in_specs = [tok(D_MODEL), tok(N_EXPERTS), tok(N_EXPERTS), pl.BlockSpec(memory_space=pl.ANY)]
    args = [x3, pos_t, aff, y]
    if final:
        in_specs.append(pl.BlockSpec((1, D_MODEL), lambda bi, ti, lo_ref: (0, 0)))
        args.append(final_g.reshape(1, D_MODEL).astype(F32))
    grid_spec = pltpu.PrefetchScalarGridSpec(
        num_scalar_prefetch=1, grid=(b, n_t), in_specs=in_specs, out_specs=tok(D_MODEL),
        scratch_shapes=[pltpu.VMEM((2, N_EXPERTS, min(tile + BF16_ROWS, cap), D_MODEL), BF16),
                        pltpu.SemaphoreType.DMA((2, N_EXPERTS))])
    return pl.pallas_call(
        functools.partial(_moe_combine_kernel, cap=cap, final=final), grid_spec=grid_spec,
        out_shape=jax.ShapeDtypeStruct(x3.shape, F32), compiler_params=_cparams(2, 48))(lo, *args)


def _moe(x_new2, hn2, aff_t, w1, w3, w2, b, seq, final_g=None):
    pos, idx, offs = _moe_select(aff_t)
    y = _moe_ffn(hn2, idx, w1, w3, w2, seq)
    tile = min(CMB_TILE, y.shape[2])
    lo = offs[:, :, ::tile // LANES].transpose(0, 2, 1).reshape(-1)
    out = _moe_combine(x_new2.reshape(b, seq, D_MODEL), pos.transpose(0, 2, 1), aff_t.transpose(0, 2, 1), lo, y,
                       final_g)
    return out.reshape(b * seq, D_MODEL)


def _dilated_kernel(q_ref, k_ref, kp_ref, kn_ref, v_ref, vp_ref, vn_ref, o_ref, l_ref, *, n_rows):
    jt = pl.program_id(2)
    jb = q_ref.shape[0]
    half = DIL_HALF
    keys = jnp.concatenate([kp_ref[...], k_ref[...], kn_ref[...]], axis=0)
    vals = jnp.concatenate([vp_ref[...], v_ref[...], vn_ref[...]], axis=0)
    sub = min(DIL_SUB, jb)
    n_keys = sub + 2 * half
    row = lax.broadcasted_iota(I32, (sub, n_keys), 0)
    col = lax.broadcasted_iota(I32, (sub, n_keys), 1)
    rel = col - half - row
    for a in range(jb // sub):
        r0 = a * sub
        kpos = jt * jb + r0 - half + col
        valid = (jnp.abs(rel) <= half) & (kpos >= 0) & (kpos < n_rows)
        qa, ka, va = q_ref[r0:r0 + sub, :], keys[r0:r0 + n_keys], vals[r0:r0 + n_keys]
        for h in range(DIL_HEADS):
            sl = slice(h * DIL_HEAD_DIM, (h + 1) * DIL_HEAD_DIM)
            s = lax.dot_general(qa[:, sl], ka[:, sl], NT_DIMS, preferred_element_type=F32)
            s = jnp.where(valid, s, NEG_INF)
            m = jnp.max(s, axis=-1, keepdims=True)
            p = jnp.exp(s - m)
            den = jnp.sum(p, axis=-1, keepdims=True)
            o = jnp.dot(p.astype(BF16), va[:, sl], preferred_element_type=F32) / den
            o_ref[r0:r0 + sub, sl] = o.astype(o_ref.dtype)
            l_ref[r0:r0 + sub, sl] = jnp.broadcast_to(m + jnp.log(den), (sub, DIL_HEAD_DIM))


def _dilated_pattern(q, k, v, gi, dilation):
    b, seq, width = q.shape
    gw = DIL_HEADS * DIL_HEAD_DIM
    n_grp = width // gw
    j = seq // dilation
    jb = min(DIL_TILE, j)
    nb = jb // DIL_HALF
    last = j // DIL_HALF - 1
    view = lambda t: t.reshape(b, j, dilation * width)
    col = lambda r: r * n_grp + gi
    main = pl.BlockSpec((None, jb, gw), lambda bi, r, jt: (bi, jt, col(r)))
    prev = pl.BlockSpec((None, DIL_HALF, gw), lambda bi, r, jt: (bi, jnp.maximum(jt * nb - 1, 0), col(r)))
    nxt = pl.BlockSpec((None, DIL_HALF, gw), lambda bi, r, jt: (bi, jnp.minimum(jt * nb + nb, last), col(r)))
    out = pl.BlockSpec((None, jb, gw), lambda bi, r, jt: (bi, jt, r))
    o, lse = pl.pallas_call(
        functools.partial(_dilated_kernel, n_rows=j), grid=(b, dilation, j // jb),
        in_specs=[main, main, prev, nxt, main, prev, nxt], out_specs=[out, out],
        out_shape=[jax.ShapeDtypeStruct((b, j, dilation * gw), BF16),
                   jax.ShapeDtypeStruct((b, j, dilation * gw), F32)],
        compiler_params=_cparams(3, 32))(view(q), view(k), view(k), view(k), view(v), view(v), view(v))
    return o.reshape(b * seq, gw), lse.reshape(b * seq, gw)


def _mlstm_kernel(q_ref, k_ref, v_ref, gc_ref, gr_ref, bc_ref, br_ref, tri_ref, trit_ref, *refs, direction, final):
    if final:
        prev_ref, og_ref, o_ref, cmem, nvec, mst = refs
    else:
        o_ref, cmem, nvec, mst = refs

    @pl.when(pl.program_id(1) == 0)
    def _():
        cmem[...] = jnp.zeros_like(cmem)
        nvec[...] = jnp.zeros_like(nvec)
        mst[...] = jnp.zeros_like(mst)

    tri = tri_ref[...]
    mask = tri > 0.5
    g_col = gc_ref[...] + bc_ref[...]
    g_row = gr_ref[...] + br_ref[...]
    lf_col, lf_row = _log_sigmoid(g_col), _log_sigmoid(g_row)
    bcum_col = jnp.dot(tri, lf_col, precision=HI, preferred_element_type=F32)
    bcum_row = jnp.dot(lf_row, trit_ref[...], precision=HI, preferred_element_type=F32)
    btot = jnp.sum(lf_col, axis=0, keepdims=True)

    for h in range(MLSTM_HEADS):
        sl = slice(h * MLSTM_DH, (h + 1) * MLSTM_DH)
        ci, cf = direction * MLSTM_HEADS + h, (2 + direction) * MLSTM_HEADS + h
        qh, kh, vh = q_ref[:, sl], k_ref[:, sl], v_ref[:, sl]
        bc, br = bcum_col[:, cf:cf + 1], bcum_row[cf:cf + 1, :]
        igc, igr = g_col[:, ci:ci + 1], g_row[ci:ci + 1, :]
        m_prev = mst[h][:, :1]
        log_d = jnp.where(mask, bc - br + igr, -jnp.inf)
        log_inter = bc + m_prev
        m_i = jnp.maximum(log_inter, jnp.max(log_d, axis=-1, keepdims=True))
        s = lax.dot_general(qh, kh, NT_DIMS, preferred_element_type=F32) * jnp.exp(log_d - m_i)
        w_inter = jnp.exp(log_inter - m_i)
        num = jnp.dot(s.astype(BF16), vh, preferred_element_type=F32)
        num = num + w_inter * jnp.dot(qh, cmem[h].astype(BF16), preferred_element_type=F32)
        den = jnp.sum(s, axis=-1, keepdims=True)
        den = den + w_inter * jnp.sum(qh.astype(F32) * nvec[h], axis=-1, keepdims=True)
        hout = num / jnp.maximum(jnp.abs(den), jnp.exp(-m_i))

        bl = btot[:, cf:cf + 1]
        m_new = jnp.maximum(bl + m_prev, jnp.max(bl - br + igr, axis=-1, keepdims=True))
        w_old = jnp.exp(bl + m_prev - m_new)
        kw = kh.astype(F32) * jnp.exp(bl - bc + igc - m_new)
        cmem[h] = w_old * cmem[h] + lax.dot_general(kw.astype(BF16), vh, TN_DIMS, preferred_element_type=F32)
        nvec[h] = w_old * nvec[h] + jnp.sum(kw, axis=0, keepdims=True)
        mst[h] = jnp.broadcast_to(m_new, (1, LANES))

        if final:
            hout = _head_norm(hout + prev_ref[:, sl])
            hout = _sigmoid(og_ref[:, sl].astype(F32)) * hout
        o_ref[:, sl] = hout.astype(o_ref.dtype)


def _mlstm_pass(q, k, v, g_col, g_row, bias_col, bias_row, direction, prev=None, ogate=None):
    b, seq, width = q.shape
    c = min(MLSTM_CHUNK, seq)
    n = seq // c
    final = prev is not None
    idx = np.arange(c)
    tri = (idx[None, :] >= idx[:, None]) if direction else (idx[None, :] <= idx[:, None])
    tri = jnp.asarray(tri, F32)
    ch = (lambda ni: n - 1 - ni) if direction else (lambda ni: ni)
    tok = lambda w: pl.BlockSpec((None, c, w), lambda bi, ni: (bi, ch(ni), 0))
    const = lambda a: pl.BlockSpec(a.shape, lambda bi, ni: (0,) * a.ndim)
    args = [q, k, v, g_col, g_row, bias_col, bias_row, tri, tri.T]
    in_specs = [tok(width)] * 3 + [tok(LANES), pl.BlockSpec((None, 4 * MLSTM_HEADS, c), lambda bi, ni: (bi, 0, ch(ni))),
                                   const(bias_col), const(bias_row), const(tri), const(tri)]
    if final:
        args += [prev, ogate]
        in_specs += [tok(width), tok(width)]
    return pl.pallas_call(
        functools.partial(_mlstm_kernel, direction=direction, final=final), grid=(b, n), in_specs=in_specs,
        out_specs=tok(width), out_shape=jax.ShapeDtypeStruct((b, seq, width), BF16 if final else F32),
        scratch_shapes=[pltpu.VMEM((MLSTM_HEADS, MLSTM_DH, MLSTM_DH), F32),
                        pltpu.VMEM((MLSTM_HEADS, 1, MLSTM_DH), F32), pltpu.VMEM((MLSTM_HEADS, 1, LANES), F32)],
        compiler_params=_cparams(2, 32))(*args)


def _odd_mixer_acts(x2, b, seq, norm_g, w_in, i_bias, f_bias):
    n_dil = len(DIL_PATTERNS) * DIL_HEADS * DIL_HEAD_DIM
    n_ml = MLSTM_HEADS * MLSTM_DH
    n_gate = 4 * MLSTM_HEADS
    cos, sin = _rope_tables(seq, DIL_HEAD_DIM // 2, 2)
    w = _split_cols(w_in, (n_dil,) * 3 + (n_ml,) * 4 + (n_gate,))
    w[-1] = jnp.pad(w[-1], ((0, 0), (0, LANES - n_gate)))
    scales = (DIL_HEAD_DIM ** -0.5, 1.0, 1.0, 1.0, MLSTM_DH ** -0.5, 1.0, 1.0, 1.0)
    cq, ck, cv, mq, mk, mv, mo, mg = _project(x2, norm_g.reshape(1, D_MODEL).astype(F32), cos, sin, w,
                                              (BF16,) * 7 + (F32,), DIL_HEAD_DIM // 2, (0, 1), scales, seq)
    sh = lambda t: t.reshape(b, seq, -1)
    dil = [_dilated_pattern(sh(cq), sh(ck), sh(cv), gi, d) for gi, (_, d) in enumerate(DIL_PATTERNS)]
    bias = jnp.concatenate([i_bias.reshape(-1), f_bias.reshape(-1)]).astype(F32)
    bias_col = jnp.pad(bias, (0, LANES - n_gate)).reshape(1, LANES)
    bias_row = bias.reshape(n_gate, 1)
    g_col = sh(mg)
    g_row = g_col[:, :, :n_gate].transpose(0, 2, 1)
    fwd = _mlstm_pass(sh(mq), sh(mk), sh(mv), g_col, g_row, bias_col, bias_row, 0)
    o_ml = _mlstm_pass(sh(mq), sh(mk), sh(mv), g_col, g_row, bias_col, bias_row, 1, prev=fwd, ogate=sh(mo))
    return dil, o_ml.reshape(b * seq, n_ml)


def kernel(x, norm_mix_g, norm_ffn_g, final_g, ev_w_in, ev_w_out, s5_a_re, s5_a_im, s5_b_re, s5_b_im, s5_c_re,
           s5_c_im, s5_log_step, s5_d, s5_glu_w, s5_glu_b, od_w_in, od_w_out, ml_i_bias, ml_f_bias, moe_router,
           moe_w1, moe_w3, moe_w2):
    b, seq, _ = x.shape
    depth = norm_mix_g.shape[0]
    row = lambda t: t.reshape(1, -1).astype(F32)
    x2 = x.reshape(b * seq, D_MODEL).astype(F32)
    for layer in range(depth):
        router = [row(norm_ffn_g[layer]), moe_router[layer].T.astype(F32)]
        if layer % 2 == 0:
            e = layer // 2
            s5_params = (s5_a_re[e], s5_a_im[e], s5_b_re[e], s5_b_im[e], s5_c_re[e], s5_c_im[e], s5_log_step[e],
                         s5_d[e])
            o_ret, y_s5 = _even_mixer_acts(x2, b, seq, norm_mix_g[layer], ev_w_in[e], s5_params)
            n_ret = o_ret.shape[1]
            consts = [ev_w_out[e][:n_ret].astype(BF16), ev_w_out[e][n_ret:].astype(BF16),
                      s5_glu_w[e].astype(BF16), row(s5_glu_b[e])] + router
            xn, hn, aff_t = _out_call(_out_even_kernel, x2, [o_ret, y_s5], consts, seq)
        else:
            o = layer // 2
            dil, o_ml = _odd_mixer_acts(x2, b, seq, norm_mix_g[layer], od_w_in[o], ml_i_bias[o], ml_f_bias[o])
            n_dil = dil[0][0].shape[1]
            consts = [od_w_out[o][:n_dil].astype(BF16), od_w_out[o][n_dil:].astype(BF16)] + router
            acts = [d[0] for d in dil] + [d[1] for d in dil] + [o_ml]
            xn, hn, aff_t = _out_call(_out_odd_kernel, x2, acts, consts, seq)
        x2 = _moe(xn, hn, aff_t, moe_w1[layer], moe_w3[layer], moe_w2[layer], b, seq,
                  final_g if layer == depth - 1 else None)
    return x2.reshape(b, seq, D_MODEL).astype(x.dtype)
```
